```python
import jax, jax.numpy as jnp
from jax import lax
import numpy as np

D_MODEL = 1024
BATCH = 8
SEQ = 2048
DEPTH = 2
DEC_BATCH = 8
DEC_SEQ = 8192
PAST_LEN = 128

D_FF = 2816
EPS = 1e-6
FNET_GROUPS = 4
FNET_GROUP_DIM = D_MODEL // 8
FNET_WIDTH = FNET_GROUPS * FNET_GROUP_DIM
CONV_CH = D_MODEL // 2
CONV_WIDTH = 31
EVEN_IN = FNET_WIDTH + 2 * CONV_CH
EVEN_MIX = FNET_WIDTH + CONV_CH
MLA_HEADS = 8
QK_NOPE = 64
QK_ROPE = 32
QK_HEAD = QK_NOPE + QK_ROPE
V_HEAD = 64
Q_LORA = 256
KV_LORA = 128
MLA_WIDTH = MLA_HEADS * V_HEAD
SC_CH = D_MODEL // 2
SC_WIDTH = 3
ODD_OFF_KV = Q_LORA
ODD_OFF_KR = Q_LORA + KV_LORA
ODD_OFF_SB = ODD_OFF_KR + QK_ROPE
ODD_OFF_SC = ODD_OFF_SB + SC_CH
ODD_OFF_SX = ODD_OFF_SC + SC_CH
ODD_IN = ODD_OFF_SX + SC_CH
ODD_MIX = MLA_WIDTH + SC_CH
ROPE_BASE = 10000.0
Q_BLOCK = 128
N_EVEN = (DEPTH + 1) // 2
N_ODD = DEPTH // 2

kernel_name = "hybrid_fnet_conformer_mla_shortconv_encoder"


def rmsnorm(x, g):
    x32 = x.astype(jnp.float32)
    y = x32 * lax.rsqrt(jnp.mean(x32 * x32, axis=-1, keepdims=True) + EPS)
    return (y * g.astype(jnp.float32)).astype(x.dtype)


def layernorm(x, g, b):
    x32 = x.astype(jnp.float32)
    mu = jnp.mean(x32, axis=-1, keepdims=True)
    xc = x32 - mu
    y = xc * lax.rsqrt(jnp.mean(xc * xc, axis=-1, keepdims=True) + EPS)
    return (y * g.astype(jnp.float32) + b.astype(jnp.float32)).astype(x.dtype)


def swiglu(h, w_gate, w_up, w_down):
    a = jnp.einsum('bsd,df->bsf', h, w_gate)
    u = jnp.einsum('bsd,df->bsf', h, w_up)
    return jnp.einsum('bsf,fd->bsd', jax.nn.silu(a) * u, w_down)


def depthwise_conv(x, w):
    k = w.shape[0]
    pad = (k - 1) // 2
    return lax.conv_general_dilated(
        x, w[:, None, :].astype(x.dtype), window_strides=(1,), padding=[(pad, pad)],
        dimension_numbers=('NWC', 'WIO', 'NWC'), feature_group_count=x.shape[-1])


def rope_tables(s):
    pos = jnp.arange(s, dtype=jnp.float32)
    inv_freq = ROPE_BASE ** (-jnp.arange(0, QK_ROPE, 2, dtype=jnp.float32) / QK_ROPE)
    ang = pos[:, None] * inv_freq[None, :]
    ang = jnp.concatenate([ang, ang], axis=-1)
    return jnp.cos(ang)[:, None, :], jnp.sin(ang)[:, None, :]


def apply_rope_tail(x, cos, sin):
    x_nope, x_rope = x[..., :QK_NOPE], x[..., QK_NOPE:]
    xr = x_rope.astype(jnp.float32)
    x1, x2 = xr[..., :QK_ROPE // 2], xr[..., QK_ROPE // 2:]
    rot = jnp.concatenate([-x2, x1], axis=-1)
    xr = xr * cos + rot * sin
    return jnp.concatenate([x_nope, xr.astype(x.dtype)], axis=-1)


def block_attention(q, k, v):
    b, s, h, dq = q.shape
    nb = s // Q_BLOCK
    scale = dq ** -0.5
    qb = q.reshape(b, nb, Q_BLOCK, h, dq).transpose(1, 0, 2, 3, 4)

    def one_block(qblk):
        sc = jnp.einsum('bqhd,bkhd->bhqk', qblk, k, preferred_element_type=jnp.float32) * scale
        p = jax.nn.softmax(sc, axis=-1)
        return jnp.einsum('bhqk,bkhd->bqhd', p.astype(v.dtype), v)

    out = lax.map(one_block, qb)
    return out.transpose(1, 0, 2, 3, 4).reshape(b, s, h * v.shape[-1])


def even_mixer(h, w_in, conv_w, conv_b, ln_g, ln_b, w_out):
    b, s, _ = h.shape
    u = jnp.einsum('bsd,de->bse', h, w_in)
    u_f, u_v, u_g = jnp.split(u, [FNET_WIDTH, FNET_WIDTH + CONV_CH], axis=-1)
    uf = u_f.reshape(b, s, FNET_GROUPS, FNET_GROUP_DIM).astype(jnp.float32)
    y_a = jnp.fft.fft2(uf, axes=(1, 3), norm='ortho').real
    y_a = y_a.reshape(b, s, FNET_WIDTH).astype(h.dtype)
    g = u_v * jax.nn.sigmoid(u_g)
    g = depthwise_conv(g, conv_w) + conv_b
    y_b = jax.nn.silu(layernorm(g, ln_g, ln_b))
    return jnp.einsum('bse,ed->bsd', jnp.concatenate([y_a, y_b], axis=-1), w_out)


def odd_mixer(h, w_in, q_norm, w_q_up, kv_norm, w_kv_up, q_head_norm, k_head_norm, sc_conv_w, w_out):
    b, s, _ = h.shape
    u = jnp.einsum('bsd,de->bse', h, w_in)
    c_q, c_kv, k_rope, sc_b, sc_c, sc_x = jnp.split(
        u, [ODD_OFF_KV, ODD_OFF_KR, ODD_OFF_SB, ODD_OFF_SC, ODD_OFF_SX], axis=-1)
    q = jnp.einsum('bsr,re->bse', rmsnorm(c_q, q_norm), w_q_up).reshape(b, s, MLA_HEADS, QK_HEAD)
    kv = jnp.einsum('bsr,re->bse', rmsnorm(c_kv, kv_norm), w_kv_up).reshape(b, s, MLA_HEADS, QK_NOPE + V_HEAD)
    k_nope, v = kv[..., :QK_NOPE], kv[..., QK_NOPE:]
    k = jnp.concatenate(
        [k_nope, jnp.broadcast_to(k_rope[:, :, None, :], (b, s, MLA_HEADS, QK_ROPE))], axis=-1)
    q = rmsnorm(q, q_head_norm)
    k = rmsnorm(k, k_head_norm)
    cos, sin = rope_tables(s)
    q = apply_rope_tail(q, cos, sin)
    k = apply_rope_tail(k, cos, sin)
    y_c = block_attention(q, k, v)
    y_d = sc_b * depthwise_conv(sc_c * sc_x, sc_conv_w)
    return jnp.einsum('bse,ed->bsd', jnp.concatenate([y_c, y_d], axis=-1), w_out)


def trunk(x, p):
    for layer in range(DEPTH):
        x = x + 0.5 * swiglu(rmsnorm(x, p['ffn1_norm'][layer]), p['ffn1_w_gate'][layer],
                             p['ffn1_w_up'][layer], p['ffn1_w_down'][layer])
        h = rmsnorm(x, p['mix_norm'][layer])
        if layer % 2 == 0:
            i = layer // 2
            x = x + even_mixer(h, p['ev_w_in'][i], p['ev_conv_w'][i], p['ev_conv_b'][i],
                               p['ev_ln_g'][i], p['ev_ln_b'][i], p['ev_w_out'][i])
        else:
            i = layer // 2
            x = x + odd_mixer(h, p['od_w_in'][i], p['od_q_norm'][i], p['od_w_q_up'][i],
                              p['od_kv_norm'][i], p['od_w_kv_up'][i], p['od_q_head_norm'][i],
                              p['od_k_head_norm'][i], p['od_sc_conv_w'][i], p['od_w_out'][i])
        x = x + 0.5 * swiglu(rmsnorm(x, p['ffn2_norm'][layer]), p['ffn2_w_gate'][layer],
                             p['ffn2_w_up'][layer], p['ffn2_w_down'][layer])
    return x


def setup_inputs(seed: int = 0) -> dict:
    key = jax.random.key(seed)
    ks = jax.random.split(key, 32)

    def w(k, shape, fan_in):
        return jax.random.normal(k, shape, jnp.float32) * (fan_in ** -0.5)

    def gain(k, shape):
        return 1.0 + 0.01 * jax.random.normal(k, shape, jnp.float32)

    def bias(k, shape):
        return 0.01 * jax.random.normal(k, shape, jnp.float32)

    return {
        'x_prompt': jax.random.normal(ks[0], (BATCH, SEQ, D_MODEL), jnp.float32),
        'x_sample': jax.random.normal(ks[1], (DEC_BATCH, DEC_SEQ, D_MODEL), jnp.float32),
        'ffn1_norm': gain(ks[2], (DEPTH, D_MODEL)),
        'ffn1_w_gate': w(ks[3], (DEPTH, D_MODEL, D_FF), D_MODEL),
        'ffn1_w_up': w(ks[4], (DEPTH, D_MODEL, D_FF), D_MODEL),
        'ffn1_w_down': w(ks[5], (DEPTH, D_FF, D_MODEL), D_FF),
        'mix_norm': gain(ks[6], (DEPTH, D_MODEL)),
        'ffn2_norm': gain(ks[7], (DEPTH, D_MODEL)),
        'ffn2_w_gate': w(ks[8], (DEPTH, D_MODEL, D_FF), D_MODEL),
        'ffn2_w_up': w(ks[9], (DEPTH, D_MODEL, D_FF), D_MODEL),
        'ffn2_w_down': w(ks[10], (DEPTH, D_FF, D_MODEL), D_FF),
        'ev_w_in': w(ks[11], (N_EVEN, D_MODEL, EVEN_IN), D_MODEL),
        'ev_conv_w': w(ks[12], (N_EVEN, CONV_WIDTH, CONV_CH), CONV_WIDTH),
        'ev_conv_b': bias(ks[13], (N_EVEN, CONV_CH)),
        'ev_ln_g': gain(ks[14], (N_EVEN, CONV_CH)),
        'ev_ln_b': bias(ks[15], (N_EVEN, CONV_CH)),
        'ev_w_out': w(ks[16], (N_EVEN, EVEN_MIX, D_MODEL), EVEN_MIX),
        'od_w_in': w(ks[17], (N_ODD, D_MODEL, ODD_IN), D_MODEL),
        'od_q_norm': gain(ks[18], (N_ODD, Q_LORA)),
        'od_w_q_up': w(ks[19], (N_ODD, Q_LORA, MLA_HEADS * QK_HEAD), Q_LORA),
        'od_kv_norm': gain(ks[20], (N_ODD, KV_LORA)),
        'od_w_kv_up': w(ks[21], (N_ODD, KV_LORA, MLA_HEADS * (QK_NOPE + V_HEAD)), KV_LORA),
        'od_q_head_norm': gain(ks[22], (N_ODD, QK_HEAD)),
        'od_k_head_norm': gain(ks[23], (N_ODD, QK_HEAD)),
        'od_sc_conv_w': w(ks[24], (N_ODD, SC_WIDTH, SC_CH), SC_WIDTH),
        'od_w_out': w(ks[25], (N_ODD, ODD_MIX, D_MODEL), ODD_MIX),
    }


def reference(x_prompt, x_sample, ffn1_norm, ffn1_w_gate, ffn1_w_up, ffn1_w_down, mix_norm,
              ffn2_norm, ffn2_w_gate, ffn2_w_up, ffn2_w_down,
              ev_w_in, ev_conv_w, ev_conv_b, ev_ln_g, ev_ln_b, ev_w_out,
              od_w_in, od_q_norm, od_w_q_up, od_kv_norm, od_w_kv_up,
              od_q_head_norm, od_k_head_norm, od_sc_conv_w, od_w_out):
    p = {
        'ffn1_norm': ffn1_norm, 'ffn1_w_gate': ffn1_w_gate, 'ffn1_w_up': ffn1_w_up,
        'ffn1_w_down': ffn1_w_down, 'mix_norm': mix_norm,
        'ffn2_norm': ffn2_norm, 'ffn2_w_gate': ffn2_w_gate, 'ffn2_w_up': ffn2_w_up,
        'ffn2_w_down': ffn2_w_down,
        'ev_w_in': ev_w_in, 'ev_conv_w': ev_conv_w, 'ev_conv_b': ev_conv_b,
        'ev_ln_g': ev_ln_g, 'ev_ln_b': ev_ln_b, 'ev_w_out': ev_w_out,
        'od_w_in': od_w_in, 'od_q_norm': od_q_norm, 'od_w_q_up': od_w_q_up,
        'od_kv_norm': od_kv_norm, 'od_w_kv_up': od_w_kv_up,
        'od_q_head_norm': od_q_head_norm, 'od_k_head_norm': od_k_head_norm,
        'od_sc_conv_w': od_sc_conv_w, 'od_w_out': od_w_out,
    }
    y_prompt = trunk(x_prompt, p)
    y_sample = trunk(x_sample, p)
    return (y_prompt, y_sample)
```

```python
import functools
import math

import jax
import jax.numpy as jnp
import numpy as np
from jax import lax
from jax.experimental import pallas as pl
from jax.experimental.pallas import tpu as pltpu

F32 = jnp.float32
BF16 = jnp.bfloat16

EPS = 1e-6
ROPE_BASE = 10000.0

FNET_GROUPS = 4
FNET_GROUP_DIM = 128
FNET_WIDTH = FNET_GROUPS * FNET_GROUP_DIM
CONV_CH = 512
CONV_WIDTH = 31
MLA_HEADS = 8
QK_NOPE = 64
QK_ROPE = 32
QK_HEAD = QK_NOPE + QK_ROPE
V_HEAD = 64
Q_LORA = 256
KV_LORA = 128
SC_CH = 512
SC_WIDTH = 3

LANES = 128
SUBLANES = 8
VMEM_LIMIT_BYTES = 56 * 1024 * 1024

ROW_TILE = 512
FF_CHUNK = 256
ATTN_TQ = 512
ATTN_TK = 256
FFT_N2 = 128
FFT_COL_TILE = 8192
FFT_K1_BLOCK = 8
CONV_HALO = 16
CONV_ROWS = 32


def _cparams(n_axes):
    return pltpu.CompilerParams(dimension_semantics=("arbitrary",) * n_axes,
                                vmem_limit_bytes=VMEM_LIMIT_BYTES)


def _const_spec(shape):
    zeros = (0,) * len(shape)
    return pl.BlockSpec(shape, lambda *_: zeros, pipeline_mode=pl.Buffered(1))


def _rmsnorm(x, g):
    ms = jnp.mean(x * x, axis=-1, keepdims=True)
    return x * lax.rsqrt(ms + EPS) * g


def _dot(a, b):
    return jnp.dot(a, b, preferred_element_type=F32)


def _ffn_kernel(x_ref, g_ref, wg_ref, wu_ref, wd_ref, o_ref, act_ref):
    x = x_ref[...]
    h = _rmsnorm(x, g_ref[...]).astype(BF16)
    d_ff = wg_ref.shape[1]
    for c in range(d_ff // FF_CHUNK):
        cols = slice(c * FF_CHUNK, (c + 1) * FF_CHUNK)
        a = _dot(h, wg_ref[:, cols])
        u = _dot(h, wu_ref[:, cols])
        act_ref[:, cols] = (a * jax.nn.sigmoid(a) * u).astype(BF16)
    o_ref[...] = x + 0.5 * _dot(act_ref[...], wd_ref[...])


def _ffn(x2d, g, wg, wu, wd):
    t, d = x2d.shape
    d_ff = wg.shape[1]
    tm = min(ROW_TILE, t)
    row = pl.BlockSpec((tm, d), lambda i: (i, 0))
    return pl.pallas_call(
        _ffn_kernel,
        grid=(t // tm,),
        in_specs=[row, _const_spec((1, d)), _const_spec((d, d_ff)), _const_spec((d, d_ff)),
                  _const_spec((d_ff, d))],
        out_specs=row,
        out_shape=jax.ShapeDtypeStruct((t, d), F32),
        scratch_shapes=[pltpu.VMEM((tm, d_ff), BF16)],
        compiler_params=_cparams(1),
        name="ffn",
    )(x2d, g, wg, wu, wd)


def _even_in_kernel(x_ref, g_ref, w_ref, cs_ref, a_ref, b_ref, glu_ref):
    h = _rmsnorm(x_ref[...], g_ref[...]).astype(BF16)
    u = _dot(h, w_ref[...])
    uf = u[:, :FNET_WIDTH].astype(BF16)
    gd = FNET_GROUP_DIM
    for j in range(FNET_GROUPS):
        ab = _dot(uf[:, j * gd:(j + 1) * gd], cs_ref[...])
        a_ref[:, j * gd:(j + 1) * gd] = ab[:, :gd].astype(BF16)
        b_ref[:, j * gd:(j + 1) * gd] = ab[:, gd:].astype(BF16)
    uv = u[:, FNET_WIDTH:FNET_WIDTH + CONV_CH]
    ug = u[:, FNET_WIDTH + CONV_CH:]
    glu_ref[...] = uv * jax.nn.sigmoid(ug)


def _even_in(x2d, g, w_in, cs):
    t, d = x2d.shape
    tm = min(ROW_TILE, t)
    row = lambda width: pl.BlockSpec((tm, width), lambda i: (i, 0))
    return pl.pallas_call(
        _even_in_kernel,
        grid=(t // tm,),
        in_specs=[row(d), _const_spec((1, d)), _const_spec(w_in.shape), _const_spec(cs.shape)],
        out_specs=[row(FNET_WIDTH), row(FNET_WIDTH), row(CONV_CH)],
        out_shape=[jax.ShapeDtypeStruct((t, FNET_WIDTH), BF16),
                   jax.ShapeDtypeStruct((t, FNET_WIDTH), BF16),
                   jax.ShapeDtypeStruct((t, CONV_CH), F32)],
        compiler_params=_cparams(1),
        name="even_in",
    )(x2d, g, w_in, cs)


def _fft1_kernel(a_ref, b_ref, ma_ref, mb_ref, o_ref):
    n1 = a_ref.shape[0]
    w = _dot(ma_ref[...], a_ref[...]) + _dot(mb_ref[...], b_ref[...])
    o_ref[0] = w[:n1].astype(BF16)
    o_ref[1] = w[n1:].astype(BF16)


def _fft1(a3, b3, ma, mb):
    bsz, n1, cols = a3.shape
    tc = min(FFT_COL_TILE, cols)
    blk = pl.BlockSpec((None, n1, tc), lambda b, j: (b, 0, j))
    return pl.pallas_call(
        _fft1_kernel,
        grid=(bsz, cols // tc),
        in_specs=[blk, blk, _const_spec(ma.shape), _const_spec(mb.shape)],
        out_specs=pl.BlockSpec((None, 2, n1, tc), lambda b, j: (b, 0, 0, j)),
        out_shape=jax.ShapeDtypeStruct((bsz, 2, n1, cols), BF16),
        compiler_params=_cparams(2),
        name="fft_stage1",
    )(a3, b3, ma, mb)


def _fft2_kernel(w_ref, g_ref, o_ref):
    k1b = g_ref.shape[0]
    c = w_ref.shape[-1]
    for j in range(k1b):
        data = jnp.concatenate([w_ref[0, j], w_ref[1, j]], axis=0)
        o_ref[:, j * c:(j + 1) * c] = _dot(g_ref[j], data).astype(o_ref.dtype)


def _fft2(w5, g):
    bsz, _, n1, n2, c = w5.shape
    k1b = min(FFT_K1_BLOCK, n1)
    return pl.pallas_call(
        _fft2_kernel,
        grid=(bsz, n1 // k1b),
        in_specs=[pl.BlockSpec((None, 2, k1b, n2, c), lambda b, j: (b, 0, j, 0, 0)),
                  pl.BlockSpec((k1b, n2, 2 * n2), lambda b, j: (j, 0, 0))],
        out_specs=pl.BlockSpec((None, n2, k1b * c), lambda b, j: (b, 0, j)),
        out_shape=jax.ShapeDtypeStruct((bsz, n2, n1 * c), BF16),
        compiler_params=_cparams(2),
        name="fft_stage2",
    )(w5, g)


def _even_out_kernel(x_ref, ya_ref, glu_ref, prev_ref, next_ref, cw_ref, cb_ref, lg_ref, lb_ref,
                     w_ref, o_ref, win_ref, cat_ref):
    i = pl.program_id(1)
    last = pl.num_programs(1) - 1
    tm, ch = glu_ref.shape
    pad = (CONV_WIDTH - 1) // 2
    win_ref[0:CONV_HALO] = jnp.where(i > 0, prev_ref[...], 0.0)
    win_ref[CONV_HALO:CONV_HALO + tm] = glu_ref[...]
    win_ref[CONV_HALO + tm:] = jnp.where(i < last, next_ref[...], 0.0)
    cat_ref[:, :FNET_WIDTH] = ya_ref[...]

    rows = min(CONV_ROWS, tm)
    first = CONV_HALO - pad

    def conv_rows(c, carry):
        base = pl.multiple_of(c * rows, rows)
        acc = jnp.broadcast_to(cb_ref[...], (rows, ch))
        for res in range(SUBLANES):
            part = None
            for blk in range((first + CONV_WIDTH - 1) // SUBLANES + 1):
                k = blk * SUBLANES + res - first
                if 0 <= k < CONV_WIDTH:
                    term = win_ref[pl.ds(base + blk * SUBLANES, rows + SUBLANES), :] * cw_ref[k:k + 1, :]
                    part = term if part is None else part + term
            acc = acc + part[res:res + rows]
        mu = jnp.mean(acc, axis=-1, keepdims=True)
        xc = acc - mu
        y = xc * lax.rsqrt(jnp.mean(xc * xc, axis=-1, keepdims=True) + EPS) * lg_ref[...] + lb_ref[...]
        cat_ref[pl.ds(base, rows), FNET_WIDTH:] = (y * jax.nn.sigmoid(y)).astype(BF16)
        return carry

    lax.fori_loop(0, tm // rows, conv_rows, 0)
    o_ref[...] = x_ref[...] + _dot(cat_ref[...], w_ref[...])


def _even_out(x, ya, glu, conv_w, conv_b, ln_g, ln_b, w_out):
    bsz, s, d = x.shape
    tm = min(ROW_TILE, s)
    hb = tm // CONV_HALO
    n_halo = s // CONV_HALO
    tile = lambda width: pl.BlockSpec((None, tm, width), lambda b, i: (b, i, 0))
    prev = pl.BlockSpec((None, CONV_HALO, CONV_CH), lambda b, i: (b, jnp.maximum(i * hb - 1, 0), 0))
    nxt = pl.BlockSpec((None, CONV_HALO, CONV_CH),
                       lambda b, i: (b, jnp.minimum((i + 1) * hb, n_halo - 1), 0))
    return pl.pallas_call(
        _even_out_kernel,
        grid=(bsz, s // tm),
        in_specs=[tile(d), tile(FNET_WIDTH), tile(CONV_CH), prev, nxt,
                  _const_spec(conv_w.shape), _const_spec((1, CONV_CH)), _const_spec((1, CONV_CH)),
                  _const_spec((1, CONV_CH)), _const_spec(w_out.shape)],
        out_specs=tile(d),
        out_shape=jax.ShapeDtypeStruct((bsz, s, d), F32),
        scratch_shapes=[pltpu.VMEM((tm + 2 * CONV_HALO, CONV_CH), F32),
                        pltpu.VMEM((tm, FNET_WIDTH + CONV_CH), BF16)],
        compiler_params=_cparams(2),
        name="even_out",
    )(x, ya, glu, glu, glu, conv_w, conv_b, ln_g, ln_b, w_out)


def _odd_in_kernel(x_ref, g_ref, w_ref, qn_ref, wq_ref, kvn_ref, wk_ref, wv_ref, qhn_ref, khn_ref,
                   cos_ref, sina_ref, sinb_ref, q_ref, k_ref, v_ref, pd_ref, sb_ref, *, q_scale):
    h = _rmsnorm(x_ref[...], g_ref[...]).astype(BF16)
    u = _dot(h, w_ref[...])
    o_kv = Q_LORA
    o_sb = o_kv + KV_LORA
    o_sc = o_sb + SC_CH
    o_sx = o_sc + SC_CH
    o_kr = o_sx + SC_CH
    sb_ref[...] = u[:, o_sb:o_sc]
    pd_ref[...] = u[:, o_sc:o_sx] * u[:, o_sx:o_kr]
    k_rope = u[:, o_kr:]

    cqn = _rmsnorm(u[:, :o_kv], qn_ref[...]).astype(BF16)
    ckvn = _rmsnorm(u[:, o_kv:o_sb], kvn_ref[...]).astype(BF16)
    q = _dot(cqn, wq_ref[...])
    kn = _dot(ckvn, wk_ref[...])
    v_ref[...] = _dot(ckvn, wv_ref[...]).astype(BF16)

    cos = cos_ref[...]
    sin_a = sina_ref[...]
    sin_b = sinb_ref[...]
    half = QK_ROPE // 2

    def head_norm_rope(t, gain):
        ms = jnp.sum(t * t, axis=-1, keepdims=True) * (1.0 / QK_HEAD)
        t = t * lax.rsqrt(ms + EPS) * gain
        return (t * cos + pltpu.roll(t, LANES - half, axis=1) * sin_a
                + pltpu.roll(t, half, axis=1) * sin_b)

    for hd in range(MLA_HEADS):
        lanes = slice(hd * LANES, (hd + 1) * LANES)
        qh = head_norm_rope(q[:, lanes], qhn_ref[...])
        q_ref[hd] = (qh * q_scale).astype(BF16)
        kh = head_norm_rope(kn[:, lanes] + k_rope, khn_ref[...])
        k_ref[hd] = kh.astype(BF16)


def _odd_in(x, g, w_in, qn, wq, kvn, wk, wv, qhn, khn, cos, sin_a, sin_b, q_scale):
    bsz, s, d = x.shape
    tm = min(ROW_TILE, s)
    tile = lambda width: pl.BlockSpec((None, tm, width), lambda b, i: (b, i, 0))
    heads = pl.BlockSpec((None, MLA_HEADS, tm, LANES), lambda b, i: (b, 0, i, 0))
    table = pl.BlockSpec((tm, LANES), lambda b, i: (i, 0))
    return pl.pallas_call(
        functools.partial(_odd_in_kernel, q_scale=q_scale),
        grid=(bsz, s // tm),
        in_specs=[tile(d), _const_spec((1, d)), _const_spec(w_in.shape), _const_spec(qn.shape),
                  _const_spec(wq.shape), _const_spec(kvn.shape), _const_spec(wk.shape),
                  _const_spec(wv.shape), _const_spec(qhn.shape), _const_spec(khn.shape),
                  table, table, table],
        out_specs=[heads, heads, tile(MLA_HEADS * V_HEAD), tile(SC_CH), tile(SC_CH)],
        out_shape=[jax.ShapeDtypeStruct((bsz, MLA_HEADS, s, LANES), BF16),
                   jax.ShapeDtypeStruct((bsz, MLA_HEADS, s, LANES), BF16),
                   jax.ShapeDtypeStruct((bsz, s, MLA_HEADS * V_HEAD), BF16),
                   jax.ShapeDtypeStruct((bsz, s, SC_CH), F32),
                   jax.ShapeDtypeStruct((bsz, s, SC_CH), F32)],
        compiler_params=_cparams(2),
        name="odd_in",
    )(x, g, w_in, qn, wq, kvn, wk, wv, qhn, khn, cos, sin_a, sin_b)


def _attn_kernel(qt_ref, k_ref, vt_ref, o_ref):
    n_chunks, _, tk = vt_ref.shape
    tq = qt_ref.shape[1]
    qt = qt_ref[...]

    def step(j, carry):
        m, l, acc = carry
        kc = k_ref[pl.ds(pl.multiple_of(j * tk, tk), tk), :]
        st = _dot(kc, qt)
        m_new = jnp.maximum(m, jnp.max(st, axis=0, keepdims=True))
        alpha = jnp.exp2(m - m_new)
        p = jnp.exp2(st - m_new)
        l = alpha * l + jnp.sum(p, axis=0, keepdims=True)
        acc = alpha * acc + _dot(vt_ref[j], p.astype(BF16))
        return m_new, l, acc

    init = (jnp.full((1, tq), -jnp.inf, F32), jnp.zeros((1, tq), F32),
            jnp.zeros((vt_ref.shape[1], tq), F32))
    _, l, acc = lax.fori_loop(0, n_chunks, step, init)
    o_ref[...] = (acc / l).astype(o_ref.dtype)


def _attention(qt, k, vt):
    bsz, nh, _, s = qt.shape
    n_chunks, dv, tk = vt.shape[2:]
    tq = min(ATTN_TQ, s)
    return pl.pallas_call(
        _attn_kernel,
        grid=(bsz, nh, s // tq),
        in_specs=[pl.BlockSpec((None, None, LANES, tq), lambda b, h, i: (b, h, 0, i)),
                  pl.BlockSpec((None, None, s, LANES), lambda b, h, i: (b, h, 0, 0)),
                  pl.BlockSpec((None, None, n_chunks, dv, tk), lambda b, h, i: (b, h, 0, 0, 0))],
        out_specs=pl.BlockSpec((None, None, dv, tq), lambda b, h, i: (b, h, 0, i)),
        out_shape=jax.ShapeDtypeStruct((bsz, nh, dv, s), BF16),
        compiler_params=_cparams(3),
        name="attention",
    )(qt, k, vt)


def _odd_out_kernel(x_ref, yc_ref, pd_ref, prev_ref, next_ref, sb_ref, cw_ref, w_ref, o_ref, cat_ref):
    i = pl.program_id(1)
    last = pl.num_programs(1) - 1
    tm = pd_ref.shape[0]
    pd = pd_ref[...]
    row = lax.broadcasted_iota(jnp.int32, pd.shape, 0)
    before = jnp.where(i > 0, prev_ref[SUBLANES - 1:SUBLANES, :], 0.0)
    after = jnp.where(i < last, next_ref[0:1, :], 0.0)
    pd_m1 = jnp.where(row == 0, before, pltpu.roll(pd, 1, axis=0))
    pd_p1 = jnp.where(row == tm - 1, after, pltpu.roll(pd, tm - 1, axis=0))
    conv = pd_m1 * cw_ref[0:1, :] + pd * cw_ref[1:2, :] + pd_p1 * cw_ref[2:3, :]
    cat_ref[:, :yc_ref.shape[1]] = yc_ref[...]
    cat_ref[:, yc_ref.shape[1]:] = (sb_ref[...] * conv).astype(BF16)
    o_ref[...] = x_ref[...] + _dot(cat_ref[...], w_ref[...])


def _odd_out(x, yc, pd, sb, conv_w, w_out):
    bsz, s, d = x.shape
    tm = min(ROW_TILE, s)
    hb = tm // SUBLANES
    n_halo = s // SUBLANES
    wc = yc.shape[-1]
    tile = lambda width: pl.BlockSpec((None, tm, width), lambda b, i: (b, i, 0))
    prev = pl.BlockSpec((None, SUBLANES, SC_CH), lambda b, i: (b, jnp.maximum(i * hb - 1, 0), 0))
    nxt = pl.BlockSpec((None, SUBLANES, SC_CH),
                       lambda b, i: (b, jnp.minimum((i + 1) * hb, n_halo - 1), 0))
    return pl.pallas_call(
        _odd_out_kernel,
        grid=(bsz, s // tm),
        in_specs=[tile(d), tile(wc), tile(SC_CH), prev, nxt, tile(SC_CH),
                  _const_spec(conv_w.shape), _const_spec(w_out.shape)],
        out_specs=tile(d),
        out_shape=jax.ShapeDtypeStruct((bsz, s, d), F32),
        scratch_shapes=[pltpu.VMEM((tm, wc + SC_CH), BF16)],
        compiler_params=_cparams(2),
        name="odd_out",
    )(x, yc, pd, pd, pd, sb, conv_w, w_out)


def _channel_dft_table():
    n = FNET_GROUP_DIM
    ang = 2.0 * np.pi * np.outer(np.arange(n), np.arange(n)) / n
    cs = np.concatenate([np.cos(ang), np.sin(ang)], axis=1) / math.sqrt(n)
    return jnp.asarray(cs, F32).astype(BF16)


def _seq_dft_tables(s):
    n2 = min(FFT_N2, s)
    n1 = s // n2
    k1 = np.arange(n1)
    ang1 = 2.0 * np.pi * np.outer(k1, k1) / n1
    c1, s1 = np.cos(ang1), np.sin(ang1)
    ma = np.concatenate([c1, -s1], axis=0)
    mb = np.concatenate([-s1, -c1], axis=0)
    k2 = np.arange(n2)
    theta = 2.0 * np.pi * (np.outer(k2, k2)[None] / n2 + (k1[:, None, None] * k2[None, None, :]) / s)
    g = np.concatenate([np.cos(theta), np.sin(theta)], axis=2) / math.sqrt(s)
    as_bf16 = lambda t: jnp.asarray(t, F32).astype(BF16)
    return n1, n2, as_bf16(ma), as_bf16(mb), as_bf16(g)


def _rope_tables(s):
    pos = jnp.arange(s, dtype=F32)
    inv_freq = ROPE_BASE ** (-jnp.arange(0, QK_ROPE, 2, dtype=F32) / QK_ROPE)
    ang = pos[:, None] * inv_freq[None, :]
    cos, sin = jnp.cos(ang), jnp.sin(ang)
    half = QK_ROPE // 2
    ones = jnp.ones((s, QK_NOPE), F32)
    zeros_n = jnp.zeros((s, QK_NOPE), F32)
    zeros_h = jnp.zeros((s, half), F32)
    tail = jnp.zeros((s, LANES - QK_HEAD), F32)
    cos_t = jnp.concatenate([ones, cos, cos, tail], axis=1)
    sin_a = jnp.concatenate([zeros_n, -sin, zeros_h, tail], axis=1)
    sin_b = jnp.concatenate([zeros_n, zeros_h, sin, tail], axis=1)
    return cos_t, sin_a, sin_b


def _pad_heads(w, per_head, width):
    k = w.shape[0]
    w = w.reshape(k, MLA_HEADS, per_head)[:, :, :width]
    return jnp.pad(w, ((0, 0), (0, 0), (0, LANES - width))).reshape(k, MLA_HEADS * LANES)


def _prep_odd(w_in, w_q_up, w_kv_up, q_head_norm, k_head_norm):
    o_kr = Q_LORA + KV_LORA
    o_sb = o_kr + QK_ROPE
    d = w_in.shape[0]
    k_rope_tile = jnp.concatenate([jnp.zeros((d, QK_NOPE), F32), w_in[:, o_kr:o_sb],
                                   jnp.zeros((d, LANES - QK_HEAD), F32)], axis=1)
    w_in_p = jnp.concatenate([w_in[:, :o_kr], w_in[:, o_sb:], k_rope_tile], axis=1).astype(BF16)
    wq = _pad_heads(w_q_up, QK_HEAD, QK_HEAD).astype(BF16)
    wk = _pad_heads(w_kv_up, QK_NOPE + V_HEAD, QK_NOPE).astype(BF16)
    wv = w_kv_up.reshape(KV_LORA, MLA_HEADS, QK_NOPE + V_HEAD)[:, :, QK_NOPE:]
    wv = wv.reshape(KV_LORA, MLA_HEADS * V_HEAD).astype(BF16)
    pad_gain = lambda g: jnp.pad(g, (0, LANES - QK_HEAD)).reshape(1, LANES)
    return w_in_p, wq, wk, wv, pad_gain(q_head_norm), pad_gain(k_head_norm)


def _even_mixer(x, p, i):
    bsz, s, d = x.shape
    a, b, glu = _even_in(x.reshape(bsz * s, d), p['mix_norm_even'][i], p['ev_w_in'][i], p['dft_cs'])
    n1, n2, ma, mb, g = p['seq_dft'][s]
    c = FNET_WIDTH
    w = _fft1(a.reshape(bsz, n1, n2 * c), b.reshape(bsz, n1, n2 * c), ma, mb)
    ya = _fft2(w.reshape(bsz, 2, n1, n2, c), g).reshape(bsz, s, c)
    return _even_out(x, ya, glu.reshape(bsz, s, CONV_CH), p['ev_conv_w'][i], p['ev_conv_b'][i],
                     p['ev_ln_g'][i], p['ev_ln_b'][i], p['ev_w_out'][i])


def _odd_mixer(x, p, i):
    bsz, s, d = x.shape
    w_in, wq, wk, wv, qhn, khn = p['odd'][i]
    cos, sin_a, sin_b = p['rope'][s]
    q_scale = QK_HEAD ** -0.5 * math.log2(math.e)
    q, k, v, pd, sb = _odd_in(x, p['mix_norm_odd'][i], w_in, p['od_q_norm'][i], wq, p['od_kv_norm'][i],
                              wk, wv, qhn, khn, cos, sin_a, sin_b, q_scale)
    tk = min(ATTN_TK, s)
    qt = jnp.swapaxes(q, 2, 3)
    vt = v.reshape(bsz, s // tk, tk, MLA_HEADS, V_HEAD).transpose(0, 3, 1, 4, 2)
    ot = _attention(qt, k, vt)
    yc = ot.transpose(0, 3, 1, 2).reshape(bsz, s, MLA_HEADS * V_HEAD)
    return _odd_out(x, yc, pd, sb, p['od_sc_conv_w'][i], p['od_w_out'][i])


def _trunk(x, p):
    bsz, s, d = x.shape
    depth = p['ffn1_norm'].shape[0]
    for layer in range(depth):
        x = _ffn(x.reshape(bsz * s, d), p['ffn1_norm'][layer], *p['ffn1'][layer]).reshape(bsz, s, d)
        x = _even_mixer(x, p, layer // 2) if layer % 2 == 0 else _odd_mixer(x, p, layer // 2)
        x = _ffn(x.reshape(bsz * s, d), p['ffn2_norm'][layer], *p['ffn2'][layer]).reshape(bsz, s, d)
    return x


def kernel(x_prompt, x_sample, ffn1_norm, ffn1_w_gate, ffn1_w_up, ffn1_w_down, mix_norm,
           ffn2_norm, ffn2_w_gate, ffn2_w_up, ffn2_w_down,
           ev_w_in, ev_conv_w, ev_conv_b, ev_ln_g, ev_ln_b, ev_w_out,
           od_w_in, od_q_norm, od_w_q_up, od_kv_norm, od_w_kv_up,
           od_q_head_norm, od_k_head_norm, od_sc_conv_w, od_w_out):
    depth, d = ffn1_norm.shape
    n_even, n_odd = ev_w_in.shape[0], od_w_in.shape[0]
    assert ev_w_in.shape[2] == FNET_WIDTH + 2 * CONV_CH and ev_conv_w.shape[1:] == (CONV_WIDTH, CONV_CH)
    assert od_w_in.shape[2] == Q_LORA + KV_LORA + QK_ROPE + 3 * SC_CH
    assert od_w_q_up.shape[1:] == (Q_LORA, MLA_HEADS * QK_HEAD)
    assert od_w_kv_up.shape[1:] == (KV_LORA, MLA_HEADS * (QK_NOPE + V_HEAD))
    row = lambda t: t.reshape(t.shape[0], 1, t.shape[1])
    bf = lambda t: t.astype(BF16)
    seqs = sorted({x_prompt.shape[1], x_sample.shape[1]})
    p = {
        'ffn1_norm': row(ffn1_norm), 'ffn2_norm': row(ffn2_norm),
        'ffn1': [(bf(ffn1_w_gate[l]), bf(ffn1_w_up[l]), bf(ffn1_w_down[l])) for l in range(depth)],
        'ffn2': [(bf(ffn2_w_gate[l]), bf(ffn2_w_up[l]), bf(ffn2_w_down[l])) for l in range(depth)],
        'mix_norm_even': row(mix_norm[0::2]), 'mix_norm_odd': row(mix_norm[1::2]),
        'ev_w_in': bf(ev_w_in), 'ev_conv_w': ev_conv_w, 'ev_conv_b': row(ev_conv_b),
        'ev_ln_g': row(ev_ln_g), 'ev_ln_b': row(ev_ln_b), 'ev_w_out': bf(ev_w_out),
        'dft_cs': _channel_dft_table(),
        'seq_dft': {s: _seq_dft_tables(s) for s in seqs},
        'rope': {s: _rope_tables(s) for s in seqs},
        'odd': [_prep_odd(od_w_in[i], od_w_q_up[i], od_w_kv_up[i], od_q_head_norm[i], od_k_head_norm[i])
                for i in range(n_odd)],
        'od_q_norm': row(od_q_norm), 'od_kv_norm': row(od_kv_norm),
        'od_sc_conv_w': od_sc_conv_w, 'od_w_out': bf(od_w_out),
    }
    del n_even
    return (_trunk(x_prompt, p), _trunk(x_sample, p))
```

```python
import functools
import math

import jax
import jax.numpy as jnp
import numpy as np
from jax import lax
from jax.experimental import pallas as pl
from jax.experimental.pallas import tpu as pltpu

F32 = jnp.float32
BF16 = jnp.bfloat16

EPS = 1e-6
ROPE_BASE = 10000.0

FNET_GROUPS = 4
FNET_GROUP_DIM = 128
FNET_WIDTH = FNET_GROUPS * FNET_GROUP_DIM
CONV_CH = 512
CONV_WIDTH = 31
MLA_HEADS = 8
QK_NOPE = 64
QK_ROPE = 32
QK_HEAD = QK_NOPE + QK_ROPE
V_HEAD = 64
Q_LORA = 256
KV_LORA = 128
SC_CH = 512
SC_WIDTH = 3

LANES = 128
SUBLANES = 8
BF16_ROWS = 16
V_ROWS = V_HEAD + BF16_ROWS
VMEM_LIMIT_BYTES = 56 * 1024 * 1024

ROW_TILE = 512
FF_CHUNK = 256
ATTN_TQ = 512
ATTN_TK = 512
FFT_N2 = 128
FFT_COL_TILE = 8192
FFT_K1_BLOCK = 8
CONV_HALO = 16
CONV_ROWS = 64


def _cparams(n_axes):
    return pltpu.CompilerParams(dimension_semantics=("arbitrary",) * n_axes,
                                vmem_limit_bytes=VMEM_LIMIT_BYTES)


def _const_spec(shape):
    zeros = (0,) * len(shape)
    return pl.BlockSpec(shape, lambda *_: zeros, pipeline_mode=pl.Buffered(1))


def _rmsnorm(x, g):
    ms = jnp.mean(x * x, axis=-1, keepdims=True)
    return x * lax.rsqrt(ms + EPS) * g


def _dot(a, b):
    return jnp.dot(a, b, preferred_element_type=F32)


def _ffn_kernel(x_ref, g_ref, wg_ref, wu_ref, wd_ref, o_ref, act_ref):
    x = x_ref[...]
    h = _rmsnorm(x, g_ref[...]).astype(BF16)
    d_ff = wg_ref.shape[1]
    for c in range(d_ff // FF_CHUNK):
        cols = slice(c * FF_CHUNK, (c + 1) * FF_CHUNK)
        a = _dot(h, wg_ref[:, cols])
        u = _dot(h, wu_ref[:, cols])
        act_ref[:, cols] = (a * jax.nn.sigmoid(a) * u).astype(BF16)
    o_ref[...] = x + 0.5 * _dot(act_ref[...], wd_ref[...])


def _ffn(x2d, g, wg, wu, wd):
    t, d = x2d.shape
    d_ff = wg.shape[1]
    tm = min(ROW_TILE, t)
    row = pl.BlockSpec((tm, d), lambda i: (i, 0))
    return pl.pallas_call(
        _ffn_kernel,
        grid=(t // tm,),
        in_specs=[row, _const_spec((1, d)), _const_spec((d, d_ff)), _const_spec((d, d_ff)),
                  _const_spec((d_ff, d))],
        out_specs=row,
        out_shape=jax.ShapeDtypeStruct((t, d), F32),
        scratch_shapes=[pltpu.VMEM((tm, d_ff), BF16)],
        compiler_params=_cparams(1),
        name="ffn",
    )(x2d, g, wg, wu, wd)


def _even_in_kernel(x_ref, g_ref, w_ref, cs_ref, a_ref, b_ref, glu_ref):
    h = _rmsnorm(x_ref[...], g_ref[...]).astype(BF16)
    u = _dot(h, w_ref[...])
    uf = u[:, :FNET_WIDTH].astype(BF16)
    gd = FNET_GROUP_DIM
    for j in range(FNET_GROUPS):
        ab = _dot(uf[:, j * gd:(j + 1) * gd], cs_ref[...])
        a_ref[:, j * gd:(j + 1) * gd] = ab[:, :gd].astype(BF16)
        b_ref[:, j * gd:(j + 1) * gd] = ab[:, gd:].astype(BF16)
    uv = u[:, FNET_WIDTH:FNET_WIDTH + CONV_CH]
    ug = u[:, FNET_WIDTH + CONV_CH:]
    glu_ref[...] = uv * jax.nn.sigmoid(ug)


def _even_in(x2d, g, w_in, cs):
    t, d = x2d.shape
    tm = min(ROW_TILE, t)
    row = lambda width: pl.BlockSpec((tm, width), lambda i: (i, 0))
    return pl.pallas_call(
        _even_in_kernel,
        grid=(t // tm,),
        in_specs=[row(d), _const_spec((1, d)), _const_spec(w_in.shape), _const_spec(cs.shape)],
        out_specs=[row(FNET_WIDTH), row(FNET_WIDTH), row(CONV_CH)],
        out_shape=[jax.ShapeDtypeStruct((t, FNET_WIDTH), BF16),
                   jax.ShapeDtypeStruct((t, FNET_WIDTH), BF16),
                   jax.ShapeDtypeStruct((t, CONV_CH), F32)],
        compiler_params=_cparams(1),
        name="even_in",
    )(x2d, g, w_in, cs)


def _fft1_kernel(a_ref, b_ref, ma_ref, mb_ref, o_ref):
    n1 = a_ref.shape[0]
    w = _dot(ma_ref[...], a_ref[...]) + _dot(mb_ref[...], b_ref[...])
    o_ref[0] = w[:n1].astype(BF16)
    o_ref[1] = w[n1:].astype(BF16)


def _fft1(a3, b3, ma, mb):
    bsz, n1, cols = a3.shape
    tc = min(FFT_COL_TILE, cols)
    blk = pl.BlockSpec((None, n1, tc), lambda b, j: (b, 0, j))
    return pl.pallas_call(
        _fft1_kernel,
        grid=(bsz, cols // tc),
        in_specs=[blk, blk, _const_spec(ma.shape), _const_spec(mb.shape)],
        out_specs=pl.BlockSpec((None, 2, n1, tc), lambda b, j: (b, 0, 0, j)),
        out_shape=jax.ShapeDtypeStruct((bsz, 2, n1, cols), BF16),
        compiler_params=_cparams(2),
        name="fft_stage1",
    )(a3, b3, ma, mb)


def _fft2_kernel(w_ref, g_ref, o_ref):
    k1b = g_ref.shape[0]
    c = w_ref.shape[-1]
    for j in range(k1b):
        data = jnp.concatenate([w_ref[0, j], w_ref[1, j]], axis=0)
        o_ref[:, j * c:(j + 1) * c] = _dot(g_ref[j], data).astype(o_ref.dtype)


def _fft2(w5, g):
    bsz, _, n1, n2, c = w5.shape
    k1b = min(FFT_K1_BLOCK, n1)
    return pl.pallas_call(
        _fft2_kernel,
        grid=(bsz, n1 // k1b),
        in_specs=[pl.BlockSpec((None, 2, k1b, n2, c), lambda b, j: (b, 0, j, 0, 0)),
                  pl.BlockSpec((k1b, n2, 2 * n2), lambda b, j: (j, 0, 0))],
        out_specs=pl.BlockSpec((None, n2, k1b * c), lambda b, j: (b, 0, j)),
        out_shape=jax.ShapeDtypeStruct((bsz, n2, n1 * c), BF16),
        compiler_params=_cparams(2),
        name="fft_stage2",
    )(w5, g)


def _even_out_kernel(x_ref, ya_ref, glu_ref, prev_ref, next_ref, cw_ref, cb_ref, lg_ref, lb_ref,
                     w_ref, o_ref, win_ref, conv_ref, cat_ref):
    i = pl.program_id(1)
    last = pl.num_programs(1) - 1
    tm, ch = glu_ref.shape
    pad = (CONV_WIDTH - 1) // 2
    win_ref[0:CONV_HALO] = jnp.where(i > 0, prev_ref[...], 0.0)
    win_ref[CONV_HALO:CONV_HALO + tm] = glu_ref[...]
    win_ref[CONV_HALO + tm:] = jnp.where(i < last, next_ref[...], 0.0)
    cat_ref[:, :FNET_WIDTH] = ya_ref[...]

    rows = min(CONV_ROWS, tm)
    first = CONV_HALO - pad

    def conv_rows(c, carry):
        base = pl.multiple_of(c * rows, rows)
        acc = jnp.broadcast_to(cb_ref[...], (rows, ch))
        for res in range(SUBLANES):
            part = None
            for blk in range((first + CONV_WIDTH - 1) // SUBLANES + 1):
                k = blk * SUBLANES + res - first
                if 0 <= k < CONV_WIDTH:
                    term = win_ref[pl.ds(base + blk * SUBLANES, rows + SUBLANES), :] * cw_ref[k:k + 1, :]
                    part = term if part is None else part + term
            acc = acc + part[res:res + rows]
        conv_ref[pl.ds(base, rows), :] = acc
        return carry

    lax.fori_loop(0, tm // rows, conv_rows, 0)
    conv = conv_ref[...]
    mu = jnp.mean(conv, axis=-1, keepdims=True)
    xc = conv - mu
    y = xc * lax.rsqrt(jnp.mean(xc * xc, axis=-1, keepdims=True) + EPS) * lg_ref[...] + lb_ref[...]
    cat_ref[:, FNET_WIDTH:] = (y * jax.nn.sigmoid(y)).astype(BF16)
    o_ref[...] = x_ref[...] + _dot(cat_ref[...], w_ref[...])


def _even_out(x, ya, glu, conv_w, conv_b, ln_g, ln_b, w_out):
    bsz, s, d = x.shape
    tm = min(ROW_TILE, s)
    hb = tm // CONV_HALO
    n_halo = s // CONV_HALO
    tile = lambda width: pl.BlockSpec((None, tm, width), lambda b, i: (b, i, 0))
    prev = pl.BlockSpec((None, CONV_HALO, CONV_CH), lambda b, i: (b, jnp.maximum(i * hb - 1, 0), 0))
    nxt = pl.BlockSpec((None, CONV_HALO, CONV_CH),
                       lambda b, i: (b, jnp.minimum((i + 1) * hb, n_halo - 1), 0))
    return pl.pallas_call(
        _even_out_kernel,
        grid=(bsz, s // tm),
        in_specs=[tile(d), tile(FNET_WIDTH), tile(CONV_CH), prev, nxt,
                  _const_spec(conv_w.shape), _const_spec((1, CONV_CH)), _const_spec((1, CONV_CH)),
                  _const_spec((1, CONV_CH)), _const_spec(w_out.shape)],
        out_specs=tile(d),
        out_shape=jax.ShapeDtypeStruct((bsz, s, d), F32),
        scratch_shapes=[pltpu.VMEM((tm + 2 * CONV_HALO, CONV_CH), F32),
                        pltpu.VMEM((tm, CONV_CH), F32),
                        pltpu.VMEM((tm, FNET_WIDTH + CONV_CH), BF16)],
        compiler_params=_cparams(2),
        name="even_out",
    )(x, ya, glu, glu, glu, conv_w, conv_b, ln_g, ln_b, w_out)


_NT = (((1,), (1,)), ((), ()))
_TN = (((0,), (0,)), ((), ()))


def _odd_in_kernel(x_ref, g_ref, w_ref, qn_ref, wqt_ref, kvn_ref, wk_ref, wvt_ref, qgain_ref,
                   kgain_ref, kgain_rot_ref, cos_ref, sin_ref, cost_ref, sint_ref,
                   qt_ref, k_ref, vt_ref, pd_ref, sb_ref, *, q_scale):
    tm = x_ref.shape[0]
    h = _rmsnorm(x_ref[...], g_ref[...]).astype(BF16)
    u = _dot(h, w_ref[...])
    o_kv = Q_LORA
    o_sb = o_kv + KV_LORA
    o_sc = o_sb + SC_CH
    o_sx = o_sc + SC_CH
    o_kr = o_sx + SC_CH
    o_rot = o_kr + LANES
    sb_ref[...] = u[:, o_sb:o_sc]
    pd_ref[...] = u[:, o_sc:o_sx] * u[:, o_sx:o_kr]
    k_rope = u[:, o_kr:o_rot]
    k_rope_rot = u[:, o_rot:]

    cqn = _rmsnorm(u[:, :o_kv], qn_ref[...]).astype(BF16)
    ckvn = _rmsnorm(u[:, o_kv:o_sb], kvn_ref[...]).astype(BF16)

    q_t = lax.dot_general(wqt_ref[...], cqn, _NT, preferred_element_type=F32)
    gain = jnp.concatenate([qgain_ref[...]] * (tm // LANES), axis=1) * q_scale
    cos_t = cost_ref[...]
    sin_t = sint_ref[...]
    half = QK_ROPE // 2
    pad_rows = jnp.zeros((LANES - QK_HEAD, tm), BF16)
    for hd in range(MLA_HEADS):
        t = q_t[hd * LANES:hd * LANES + QK_HEAD]
        ms = jnp.sum(t * t, axis=0, keepdims=True) * (1.0 / QK_HEAD)
        t = t * lax.rsqrt(ms + EPS) * gain
        x1 = t[QK_NOPE:QK_NOPE + half]
        x2 = t[QK_NOPE + half:]
        qt_ref[hd, 0:QK_NOPE] = t[:QK_NOPE].astype(BF16)
        qt_ref[hd, QK_NOPE:QK_NOPE + half] = (x1 * cos_t - x2 * sin_t).astype(BF16)
        qt_ref[hd, QK_NOPE + half:QK_HEAD] = (x2 * cos_t + x1 * sin_t).astype(BF16)
        qt_ref[hd, QK_HEAD:] = pad_rows

    kn = _dot(ckvn, wk_ref[...])
    k_cos = cos_ref[...] * kgain_ref[...]
    k_rot = k_rope_rot * (sin_ref[...] * kgain_rot_ref[...])
    for hd in range(MLA_HEADS):
        kh = kn[:, hd * LANES:(hd + 1) * LANES] + k_rope
        ms = jnp.sum(kh * kh, axis=-1, keepdims=True) * (1.0 / QK_HEAD)
        k_ref[hd] = ((kh * k_cos + k_rot) * lax.rsqrt(ms + EPS)).astype(BF16)

    v_t = lax.dot_general(wvt_ref[...], ckvn, _NT, preferred_element_type=F32)
    for hd in range(MLA_HEADS):
        vt_ref[hd, :V_HEAD] = v_t[hd * V_HEAD:(hd + 1) * V_HEAD].astype(BF16)
        vt_ref[hd, V_HEAD:] = jnp.ones((BF16_ROWS, tm), BF16)


def _odd_in(x, g, w_in, qn, wqt, kvn, wk, wvt, qgain, kgain, kgain_rot, tables, q_scale):
    bsz, s, d = x.shape
    cos, sin, cos_t, sin_t = tables
    tm = min(ROW_TILE, s)
    tile = lambda width: pl.BlockSpec((None, tm, width), lambda b, i: (b, i, 0))
    heads_t = lambda rows: pl.BlockSpec((None, MLA_HEADS, rows, tm), lambda b, i: (b, 0, 0, i))
    consts = [g, w_in, qn, wqt, kvn, wk, wvt, qgain, kgain, kgain_rot]
    return pl.pallas_call(
        functools.partial(_odd_in_kernel, q_scale=q_scale),
        grid=(bsz, s // tm),
        in_specs=[tile(d)] + [_const_spec(c.shape) for c in consts]
                 + [pl.BlockSpec((tm, LANES), lambda b, i: (i, 0))] * 2
                 + [pl.BlockSpec((QK_ROPE // 2, tm), lambda b, i: (0, i))] * 2,
        out_specs=[heads_t(LANES),
                   pl.BlockSpec((None, MLA_HEADS, tm, LANES), lambda b, i: (b, 0, i, 0)),
                   heads_t(V_ROWS), tile(SC_CH), tile(SC_CH)],
        out_shape=[jax.ShapeDtypeStruct((bsz, MLA_HEADS, LANES, s), BF16),
                   jax.ShapeDtypeStruct((bsz, MLA_HEADS, s, LANES), BF16),
                   jax.ShapeDtypeStruct((bsz, MLA_HEADS, V_ROWS, s), BF16),
                   jax.ShapeDtypeStruct((bsz, s, SC_CH), F32),
                   jax.ShapeDtypeStruct((bsz, s, SC_CH), F32)],
        compiler_params=_cparams(2),
        name="odd_in",
    )(x, *consts, cos, sin, cos_t, sin_t)


def _attn_kernel(qt_ref, k_ref, vt_ref, o_ref, st0_ref, st1_ref, p0_ref, p1_ref):
    st_refs = (st0_ref, st1_ref)
    p_refs = (p0_ref, p1_ref)
    tk, tq = st0_ref.shape
    dva, s = vt_ref.shape
    dv = o_ref.shape[0]
    n_chunks = s // tk
    qt = qt_ref[...]

    def keys(j):
        return pl.ds(pl.multiple_of(jnp.clip(j, 0, n_chunks - 1) * tk, tk), tk)

    def scores(j, slot):
        st = _dot(k_ref[keys(j), :], qt)
        st_refs[slot][...] = st
        return jnp.max(st, axis=0, keepdims=True)

    def pv(j, slot, alpha, acc):
        return alpha * acc + _dot(vt_ref[:, keys(j)], p_refs[slot][...])

    def step(j, slot, carry):
        m, cmax, alpha_prev, acc = carry
        acc = pv(j - 1, 1 - slot, alpha_prev, acc)
        m_new = jnp.maximum(m, cmax)
        p_refs[slot][...] = jnp.exp2(st_refs[slot][...] - m_new).astype(BF16)
        cmax_next = scores(j + 1, 1 - slot)
        return m_new, cmax_next, jnp.exp2(m - m_new), acc

    def pair(u, carry):
        return step(2 * u + 1, 1, step(2 * u, 0, carry))

    p1_ref[...] = jnp.zeros((tk, tq), BF16)
    carry = (jnp.full((1, tq), -jnp.inf, F32), scores(0, 0), jnp.ones((1, tq), F32),
             jnp.zeros((dva, tq), F32))
    carry = lax.fori_loop(0, n_chunks // 2, pair, carry)
    if n_chunks % 2:
        carry = step(n_chunks - 1, 0, carry)
    _, _, alpha, acc = carry
    acc = pv(n_chunks - 1, (n_chunks - 1) % 2, alpha, acc)
    o_ref[...] = (acc[:dv] / acc[dv:dv + 1]).astype(o_ref.dtype)


def _attention(qt, k, vt):
    bsz, nh, _, s = qt.shape
    dva = vt.shape[2]
    tq = min(ATTN_TQ, s)
    tk = min(ATTN_TK, s)
    return pl.pallas_call(
        _attn_kernel,
        grid=(bsz, nh, s // tq),
        in_specs=[pl.BlockSpec((None, None, LANES, tq), lambda b, h, i: (b, h, 0, i)),
                  pl.BlockSpec((None, None, s, LANES), lambda b, h, i: (b, h, 0, 0)),
                  pl.BlockSpec((None, None, dva, s), lambda b, h, i: (b, h, 0, 0))],
        out_specs=pl.BlockSpec((None, None, V_HEAD, tq), lambda b, h, i: (b, h, 0, i)),
        out_shape=jax.ShapeDtypeStruct((bsz, nh, V_HEAD, s), BF16),
        scratch_shapes=[pltpu.VMEM((tk, tq), F32)] * 2 + [pltpu.VMEM((tk, tq), BF16)] * 2,
        compiler_params=_cparams(3),
        name="attention",
    )(qt, k, vt)


def _odd_out_kernel(x_ref, yct_ref, pd_ref, prev_ref, next_ref, sb_ref, cw_ref, wc_ref, wd_ref, o_ref):
    i = pl.program_id(1)
    last = pl.num_programs(1) - 1
    tm = pd_ref.shape[0]
    pd = pd_ref[...]
    row = lax.broadcasted_iota(jnp.int32, pd.shape, 0)
    before = jnp.where(i > 0, prev_ref[SUBLANES - 1:SUBLANES, :], 0.0)
    after = jnp.where(i < last, next_ref[0:1, :], 0.0)
    pd_m1 = jnp.where(row == 0, before, pltpu.roll(pd, 1, axis=0))
    pd_p1 = jnp.where(row == tm - 1, after, pltpu.roll(pd, tm - 1, axis=0))
    conv = pd_m1 * cw_ref[0:1, :] + pd * cw_ref[1:2, :] + pd_p1 * cw_ref[2:3, :]
    yd = (sb_ref[...] * conv).astype(BF16)
    yc_w = lax.dot_general(yct_ref[...], wc_ref[...], _TN, preferred_element_type=F32)
    o_ref[...] = x_ref[...] + yc_w + _dot(yd, wd_ref[...])


def _odd_out(x, yct, pd, sb, conv_w, w_c, w_d):
    bsz, s, d = x.shape
    tm = min(ROW_TILE, s)
    hb = tm // SUBLANES
    n_halo = s // SUBLANES
    wc = yct.shape[1]
    tile = lambda width: pl.BlockSpec((None, tm, width), lambda b, i: (b, i, 0))
    prev = pl.BlockSpec((None, SUBLANES, SC_CH), lambda b, i: (b, jnp.maximum(i * hb - 1, 0), 0))
    nxt = pl.BlockSpec((None, SUBLANES, SC_CH),
                       lambda b, i: (b, jnp.minimum((i + 1) * hb, n_halo - 1), 0))
    return pl.pallas_call(
        _odd_out_kernel,
        grid=(bsz, s // tm),
        in_specs=[tile(d), pl.BlockSpec((None, wc, tm), lambda b, i: (b, 0, i)), tile(SC_CH), prev, nxt,
                  tile(SC_CH), _const_spec(conv_w.shape), _const_spec(w_c.shape), _const_spec(w_d.shape)],
        out_specs=tile(d),
        out_shape=jax.ShapeDtypeStruct((bsz, s, d), F32),
        compiler_params=_cparams(2),
        name="odd_out",
    )(x, yct, pd, pd, pd, sb, conv_w, w_c, w_d)


def _channel_dft_table():
    n = FNET_GROUP_DIM
    ang = 2.0 * np.pi * np.outer(np.arange(n), np.arange(n)) / n
    cs = np.concatenate([np.cos(ang), np.sin(ang)], axis=1) / math.sqrt(n)
    return jnp.asarray(cs, F32).astype(BF16)


def _seq_dft_tables(s):
    n2 = min(FFT_N2, s)
    n1 = s // n2
    k1 = np.arange(n1)
    ang1 = 2.0 * np.pi * np.outer(k1, k1) / n1
    c1, s1 = np.cos(ang1), np.sin(ang1)
    ma = np.concatenate([c1, -s1], axis=0)
    mb = np.concatenate([-s1, -c1], axis=0)
    k2 = np.arange(n2)
    theta = 2.0 * np.pi * (np.outer(k2, k2)[None] / n2 + (k1[:, None, None] * k2[None, None, :]) / s)
    g = np.concatenate([np.cos(theta), np.sin(theta)], axis=2) / math.sqrt(s)
    as_bf16 = lambda t: jnp.asarray(t, F32).astype(BF16)
    return n1, n2, as_bf16(ma), as_bf16(mb), as_bf16(g)


def _rope_tables(s):
    pos = jnp.arange(s, dtype=F32)
    inv_freq = ROPE_BASE ** (-jnp.arange(0, QK_ROPE, 2, dtype=F32) / QK_ROPE)
    ang = pos[:, None] * inv_freq[None, :]
    cos, sin = jnp.cos(ang), jnp.sin(ang)
    ones = jnp.ones((s, QK_NOPE), F32)
    zeros_n = jnp.zeros((s, QK_NOPE), F32)
    tail = jnp.zeros((s, LANES - QK_HEAD), F32)
    cos_k = jnp.concatenate([ones, cos, cos, tail], axis=1)
    sin_k = jnp.concatenate([zeros_n, sin, sin, tail], axis=1)
    return cos_k, sin_k, cos.T, sin.T


def _pad_heads(w, per_head, width):
    k = w.shape[0]
    w = w.reshape(k, MLA_HEADS, per_head)[:, :, :width]
    return jnp.pad(w, ((0, 0), (0, 0), (0, LANES - width))).reshape(k, MLA_HEADS * LANES)


def _rope_tile(lo, hi):
    rows = lo.shape[:-1]
    return jnp.concatenate([jnp.zeros(rows + (QK_NOPE,), F32), lo, hi,
                            jnp.zeros(rows + (LANES - QK_HEAD,), F32)], axis=-1)


def _prep_odd(w_in, w_q_up, w_kv_up, q_head_norm, k_head_norm, w_out):
    o_kr = Q_LORA + KV_LORA
    o_sb = o_kr + QK_ROPE
    half = QK_ROPE // 2
    r1, r2 = w_in[:, o_kr:o_kr + half], w_in[:, o_kr + half:o_sb]
    w_in_p = jnp.concatenate([w_in[:, :o_kr], w_in[:, o_sb:], _rope_tile(r1, r2), _rope_tile(-r2, r1)],
                             axis=1).astype(BF16)
    wqt = _pad_heads(w_q_up, QK_HEAD, QK_HEAD).T.astype(BF16)
    wk = _pad_heads(w_kv_up, QK_NOPE + V_HEAD, QK_NOPE).astype(BF16)
    wv = w_kv_up.reshape(KV_LORA, MLA_HEADS, QK_NOPE + V_HEAD)[:, :, QK_NOPE:]
    wvt = wv.reshape(KV_LORA, MLA_HEADS * V_HEAD).T.astype(BF16)
    qgain = jnp.broadcast_to(q_head_norm[:, None], (QK_HEAD, LANES))
    kg = k_head_norm
    kgain = jnp.pad(kg, (0, LANES - QK_HEAD)).reshape(1, LANES)
    kgain_rot = _rope_tile(kg[QK_NOPE + half:], kg[QK_NOPE:QK_NOPE + half]).reshape(1, LANES)
    wc = MLA_HEADS * V_HEAD
    return (w_in_p, wqt, wk, wvt, qgain, kgain, kgain_rot,
            w_out[:wc].astype(BF16), w_out[wc:].astype(BF16))


def _even_mixer(x, p, i):
    bsz, s, d = x.shape
    a, b, glu = _even_in(x.reshape(bsz * s, d), p['mix_norm_even'][i], p['ev_w_in'][i], p['dft_cs'])
    n1, n2, ma, mb, g = p['seq_dft'][s]
    c = FNET_WIDTH
    w = _fft1(a.reshape(bsz, n1, n2 * c), b.reshape(bsz, n1, n2 * c), ma, mb)
    ya = _fft2(w.reshape(bsz, 2, n1, n2, c), g).reshape(bsz, s, c)
    return _even_out(x, ya, glu.reshape(bsz, s, CONV_CH), p['ev_conv_w'][i], p['ev_conv_b'][i],
                     p['ev_ln_g'][i], p['ev_ln_b'][i], p['ev_w_out'][i])


def _odd_mixer(x, p, i):
    bsz, s, d = x.shape
    w_in, wqt, wk, wvt, qgain, kgain, kgain_rot, w_c, w_d = p['odd'][i]
    q_scale = QK_HEAD ** -0.5 * math.log2(math.e)
    qt, k, vt, pd, sb = _odd_in(x, p['mix_norm_odd'][i], w_in, p['od_q_norm'][i], wqt, p['od_kv_norm'][i],
                                wk, wvt, qgain, kgain, kgain_rot, p['rope'][s], q_scale)
    ot = _attention(qt, k, vt)
    yct = ot.reshape(bsz, MLA_HEADS * V_HEAD, s)
    return _odd_out(x, yct, pd, sb, p['od_sc_conv_w'][i], w_c, w_d)


def _trunk(x, p):
    bsz, s, d = x.shape
    depth = p['ffn1_norm'].shape[0]
    for layer in range(depth):
        x = _ffn(x.reshape(bsz * s, d), p['ffn1_norm'][layer], *p['ffn1'][layer]).reshape(bsz, s, d)
        x = _even_mixer(x, p, layer // 2) if layer % 2 == 0 else _odd_mixer(x, p, layer // 2)
        x = _ffn(x.reshape(bsz * s, d), p['ffn2_norm'][layer], *p['ffn2'][layer]).reshape(bsz, s, d)
    return x


def kernel(x_prompt, x_sample, ffn1_norm, ffn1_w_gate, ffn1_w_up, ffn1_w_down, mix_norm,
           ffn2_norm, ffn2_w_gate, ffn2_w_up, ffn2_w_down,
           ev_w_in, ev_conv_w, ev_conv_b, ev_ln_g, ev_ln_b, ev_w_out,
           od_w_in, od_q_norm, od_w_q_up, od_kv_norm, od_w_kv_up,
           od_q_head_norm, od_k_head_norm, od_sc_conv_w, od_w_out):
    depth, d = ffn1_norm.shape
    n_even, n_odd = ev_w_in.shape[0], od_w_in.shape[0]
    assert ev_w_in.shape[2] == FNET_WIDTH + 2 * CONV_CH and ev_conv_w.shape[1:] == (CONV_WIDTH, CONV_CH)
    assert od_w_in.shape[2] == Q_LORA + KV_LORA + QK_ROPE + 3 * SC_CH
    assert od_w_q_up.shape[1:] == (Q_LORA, MLA_HEADS * QK_HEAD)
    assert od_w_kv_up.shape[1:] == (KV_LORA, MLA_HEADS * (QK_NOPE + V_HEAD))
    row = lambda t: t.reshape(t.shape[0], 1, t.shape[1])
    bf = lambda t: t.astype(BF16)
    seqs = sorted({x_prompt.shape[1], x_sample.shape[1]})
    p = {
        'ffn1_norm': row(ffn1_norm), 'ffn2_norm': row(ffn2_norm),
        'ffn1': [(bf(ffn1_w_gate[l]), bf(ffn1_w_up[l]), bf(ffn1_w_down[l])) for l in range(depth)],
        'ffn2': [(bf(ffn2_w_gate[l]), bf(ffn2_w_up[l]), bf(ffn2_w_down[l])) for l in range(depth)],
        'mix_norm_even': row(mix_norm[0::2]), 'mix_norm_odd': row(mix_norm[1::2]),
        'ev_w_in': bf(ev_w_in), 'ev_conv_w': ev_conv_w, 'ev_conv_b': row(ev_conv_b),
        'ev_ln_g': row(ev_ln_g), 'ev_ln_b': row(ev_ln_b), 'ev_w_out': bf(ev_w_out),
        'dft_cs': _channel_dft_table(),
        'seq_dft': {s: _seq_dft_tables(s) for s in seqs},
        'rope': {s: _rope_tables(s) for s in seqs},
        'odd': [_prep_odd(od_w_in[i], od_w_q_up[i], od_w_kv_up[i], od_q_head_norm[i], od_k_head_norm[i],
                          od_w_out[i]) for i in range(n_odd)],
        'od_q_norm': row(od_q_norm), 'od_kv_norm': row(od_kv_norm),
        'od_sc_conv_w': od_sc_conv_w,
    }
    del n_even
    return (_trunk(x_prompt, p), _trunk(x_sample, p))
```
